```python
import math
import jax, jax.numpy as jnp
from jax import lax
import numpy as np

D_MODEL = 2048
BATCH = 1
SEQ = 16384
DEPTH = 1

GRID_W = 64
CTX_LEN = 256
A_HEADS = 8
A_QK_DIM = 64
A_V_DIM = 2 * A_QK_DIM
ROPE_THETA = 10000.0
B_HEADS = 8
B_HEAD_DIM = 128
NA_KH = 8
NA_KW = 16
A_WIDTH = A_HEADS * A_V_DIM
B_WIDTH = B_HEADS * B_HEAD_DIM
IN_SIZES = (A_HEADS * 2 * A_QK_DIM, A_HEADS * 2 * A_QK_DIM, A_WIDTH,
            B_WIDTH, B_WIDTH, B_WIDTH, D_MODEL, D_MODEL)
N_IN = sum(IN_SIZES)
PEER_HEADS = 8
PEER_NKEYS = 128
PEER_N = PEER_NKEYS * PEER_NKEYS
PEER_TOPK = 16
PEER_QDIM = 256
PEER_CHUNK = 128
Q_BLOCK = 128
EPS = 1e-6

kernel_name = "hybrid_diffattn_natten_peer_dit_block"


def rmsnorm(x, g):
    xf = x.astype(jnp.float32)
    y = xf * lax.rsqrt(jnp.mean(xf * xf, axis=-1, keepdims=True) + EPS)
    return (y * g.astype(jnp.float32)).astype(x.dtype)


def ada_params(cond, w, b):
    m = jax.nn.silu(cond) @ w + b
    return [t[:, None, :] for t in jnp.split(m, 6, axis=-1)]


def modulate(x, g, shift, scale):
    return rmsnorm(x, g) * (1 + scale) + shift


def split_proj(p):
    B, T, _ = p.shape
    offs = [int(o) for o in np.cumsum(IN_SIZES)[:-1]]
    qa, ka, va, qb, kb, vb, ga, gb = jnp.split(p, offs, axis=-1)
    qa = qa.reshape(B, T, A_HEADS, 2, A_QK_DIM)
    ka = ka.reshape(B, T, A_HEADS, 2, A_QK_DIM)
    va = va.reshape(B, T, A_HEADS, A_V_DIM)
    qb = qb.reshape(B, T, B_HEADS, B_HEAD_DIM)
    kb = kb.reshape(B, T, B_HEADS, B_HEAD_DIM)
    vb = vb.reshape(B, T, B_HEADS, B_HEAD_DIM)
    return qa, ka, va, qb, kb, vb, ga, gb


def axial_rope_tables(seq_len, dtype):
    t = jnp.arange(seq_len)
    row = (t // GRID_W).astype(jnp.float32)
    col = (t % GRID_W).astype(jnp.float32)
    n_freq = A_QK_DIM // 4
    freqs = ROPE_THETA ** (-jnp.arange(n_freq, dtype=jnp.float32) / n_freq)
    ar = row[:, None] * freqs[None, :]
    ac = col[:, None] * freqs[None, :]
    ang = jnp.concatenate([ar, ar, ac, ac], axis=-1)
    return jnp.cos(ang).astype(dtype), jnp.sin(ang).astype(dtype)


def rotate_half_axial(x):
    xr = x.reshape(x.shape[:-1] + (2, 2, A_QK_DIM // 4))
    xr = jnp.stack([-xr[..., 1, :], xr[..., 0, :]], axis=-2)
    return xr.reshape(x.shape)


def apply_rope(x, cos, sin):
    c = cos[None, :, None, None, :]
    s = sin[None, :, None, None, :]
    return x * c + rotate_half_axial(x) * s


def diff_attention(q, k, v, lam, lam_init, g):
    B, T, H, _, d = q.shape
    nb = T // Q_BLOCK
    scale = d ** -0.5
    qb = q.reshape(B, nb, Q_BLOCK, H, 2, d).swapaxes(0, 1)

    def block(qblk):
        s = jnp.einsum('bqhid,bkhid->ibhqk', qblk, k).astype(jnp.float32) * scale
        p = jax.nn.softmax(s, axis=-1)
        a = p[0] - lam * p[1]
        return jnp.einsum('bhqk,bkhe->bqhe', a.astype(v.dtype), v)

    o = lax.map(block, qb)
    o = o.swapaxes(0, 1).reshape(B, T, H, v.shape[-1])
    o = rmsnorm(o, g) * (1.0 - lam_init)
    return o.reshape(B, T, H * v.shape[-1])


def neighbourhood_attention(q, k, v, kc, vc, rpb):
    B, T, H, d = q.shape
    rows = T // GRID_W
    kh = min(NA_KH, rows)
    kw = NA_KW
    scale = d ** -0.5
    qg = q.reshape(B, rows, GRID_W, H, d)
    kg = k.reshape(B, rows, GRID_W, H, d)
    vg = v.reshape(B, rows, GRID_W, H, d)
    cols = jnp.arange(GRID_W)
    col_start = jnp.clip(cols - kw // 2, 0, GRID_W - kw)
    col_idx = col_start[:, None] + jnp.arange(kw)[None, :]
    bias_c = rpb[:, :, col_idx - cols[:, None] + (NA_KW - 1)]

    def row(r):
        rs = jnp.clip(r - kh // 2, 0, rows - kh)
        dr = rs + jnp.arange(kh) - r
        bias = bias_c[:, dr + (NA_KH - 1)].transpose(0, 2, 1, 3)
        q_r = lax.dynamic_index_in_dim(qg, r, axis=1, keepdims=False)
        k_win = lax.dynamic_slice_in_dim(kg, rs, kh, axis=1)[:, :, col_idx]
        v_win = lax.dynamic_slice_in_dim(vg, rs, kh, axis=1)[:, :, col_idx]
        s_win = jnp.einsum('bqhd,biqjhd->bhqij', q_r, k_win).astype(jnp.float32) * scale
        s_win = s_win + bias[None].astype(jnp.float32)
        s_ctx = jnp.einsum('bqhd,bchd->bhqc', q_r, kc).astype(jnp.float32) * scale
        s = jnp.concatenate([s_win.reshape(B, H, GRID_W, kh * kw), s_ctx], axis=-1)
        p = jax.nn.softmax(s, axis=-1).astype(v.dtype)
        p_win = p[..., :kh * kw].reshape(B, H, GRID_W, kh, kw)
        p_ctx = p[..., kh * kw:]
        return (jnp.einsum('bhqij,biqjhd->bqhd', p_win, v_win)
                + jnp.einsum('bhqc,bchd->bqhd', p_ctx, vc))

    o = lax.map(row, jnp.arange(rows))
    return o.transpose(1, 0, 2, 3, 4).reshape(B, T, H * d)


def full_attention(q, k, v):
    B, T, H, d = q.shape
    s = jnp.einsum('bqhd,bkhd->bhqk', q, k).astype(jnp.float32) * (d ** -0.5)
    p = jax.nn.softmax(s, axis=-1).astype(v.dtype)
    return jnp.einsum('bhqk,bkhd->bqhd', p, v).reshape(B, T, H * d)


def merge(out_a, out_b, ga, gb, w_a, w_b, w_o):
    return (jax.nn.sigmoid(ga) * (out_a @ w_a) + jax.nn.sigmoid(gb) * (out_b @ w_b)) @ w_o


def peer(h, w_q, subkeys, u, v):
    B, T, D = h.shape
    n = B * T
    tok = h.reshape(n, D)
    q = (tok @ w_q).reshape(n, PEER_HEADS, 2, PEER_QDIM // 2)
    s = jnp.einsum('nhpd,hpkd->nhpk', q, subkeys).astype(jnp.float32)
    s1, i1 = lax.top_k(s[:, :, 0], PEER_TOPK)
    s2, i2 = lax.top_k(s[:, :, 1], PEER_TOPK)
    cand = (s1[..., :, None] + s2[..., None, :]).reshape(n, PEER_HEADS, PEER_TOPK * PEER_TOPK)
    cidx = (i1[..., :, None] * PEER_NKEYS + i2[..., None, :]).reshape(n, PEER_HEADS, PEER_TOPK * PEER_TOPK)
    best, pos = lax.top_k(cand, PEER_TOPK)
    idx = jnp.take_along_axis(cidx, pos, axis=-1)
    gates = jax.nn.softmax(best, axis=-1).astype(h.dtype)
    nc = n // PEER_CHUNK

    def chunk(args):
        t, ix, gg = args
        act = jax.nn.gelu(jnp.einsum('nd,nhkd->nhk', t, u[ix]))
        return jnp.einsum('nhk,nhkd->nd', gg * act, v[ix])

    out = lax.map(chunk, (tok.reshape(nc, PEER_CHUNK, D),
                          idx.reshape(nc, PEER_CHUNK, PEER_HEADS, PEER_TOPK),
                          gates.reshape(nc, PEER_CHUNK, PEER_HEADS, PEER_TOPK)))
    return out.reshape(B, T, D)


def setup_inputs(seed: int = 0) -> dict:
    key = jax.random.key(seed)
    ks = jax.random.split(key, 24)
    D = D_MODEL

    def nrm(k, shape, s):
        return jax.random.normal(k, shape, jnp.float32) * s

    return {
        'x': nrm(ks[0], (BATCH, SEQ, D), 1.0),
        'c': nrm(ks[1], (BATCH, D), 1.0),
        'ctx': nrm(ks[2], (BATCH, CTX_LEN, D), 1.0),
        'c_ctx': nrm(ks[3], (D,), 1.0),
        'w_mod': nrm(ks[4], (DEPTH, D, 6 * D), 0.5 * D ** -0.5),
        'b_mod': nrm(ks[5], (DEPTH, 6 * D), 0.01),
        'norm1_g': 1.0 + nrm(ks[6], (DEPTH, D), 0.02),
        'norm2_g': 1.0 + nrm(ks[7], (DEPTH, D), 0.02),
        'w_in': nrm(ks[8], (DEPTH, D, N_IN), D ** -0.5),
        'lambda_q1': nrm(ks[9], (DEPTH, A_QK_DIM), 0.1),
        'lambda_k1': nrm(ks[10], (DEPTH, A_QK_DIM), 0.1),
        'lambda_q2': nrm(ks[11], (DEPTH, A_QK_DIM), 0.1),
        'lambda_k2': nrm(ks[12], (DEPTH, A_QK_DIM), 0.1),
        'subln_g': 1.0 + nrm(ks[13], (DEPTH, A_V_DIM), 0.02),
        'na_rpb': nrm(ks[14], (DEPTH, B_HEADS, 2 * NA_KH - 1, 2 * NA_KW - 1), 0.1),
        'w_branch_a': nrm(ks[15], (DEPTH, A_WIDTH, D), A_WIDTH ** -0.5),
        'w_branch_b': nrm(ks[16], (DEPTH, B_WIDTH, D), B_WIDTH ** -0.5),
        'w_out': nrm(ks[17], (DEPTH, D, D), D ** -0.5),
        'peer_w_q': nrm(ks[18], (DEPTH, D, PEER_HEADS * PEER_QDIM), D ** -0.5),
        'peer_subkeys': nrm(ks[19], (DEPTH, PEER_HEADS, 2, PEER_NKEYS, PEER_QDIM // 2), (PEER_QDIM // 2) ** -0.5),
        'peer_u': nrm(ks[20], (DEPTH, PEER_N, D), D ** -0.5),
        'peer_v': nrm(ks[21], (DEPTH, PEER_N, D), PEER_HEADS ** -0.5),
        'final_g': 1.0 + nrm(ks[22], (D,), 0.02),
    }


def reference(x, c, ctx, c_ctx, w_mod, b_mod, norm1_g, norm2_g, w_in,
              lambda_q1, lambda_k1, lambda_q2, lambda_k2, subln_g, na_rpb,
              w_branch_a, w_branch_b, w_out, peer_w_q, peer_subkeys, peer_u, peer_v,
              final_g):
    S = x.shape[1]
    cos, sin = axial_rope_tables(S, x.dtype)
    for l in range(DEPTH):
        lam_init = 0.8 - 0.6 * math.exp(-0.3 * l)
        lam = (jnp.exp(jnp.sum(lambda_q1[l].astype(jnp.float32) * lambda_k1[l].astype(jnp.float32)))
               - jnp.exp(jnp.sum(lambda_q2[l].astype(jnp.float32) * lambda_k2[l].astype(jnp.float32)))
               + lam_init)
        sh1, sc1, gt1, sh2, sc2, gt2 = ada_params(c, w_mod[l], b_mod[l])
        sh1c, sc1c, gt1c, sh2c, sc2c, gt2c = ada_params(c_ctx[None, :], w_mod[l], b_mod[l])

        hx = modulate(x, norm1_g[l], sh1, sc1)
        hc = modulate(ctx, norm1_g[l], sh1c, sc1c)
        qa, ka, va, qb, kb, vb, ga, gb = split_proj(hx @ w_in[l])
        qa_c, ka_c, va_c, qb_c, kb_c, vb_c, ga_c, gb_c = split_proj(hc @ w_in[l])
        qa = apply_rope(qa, cos, sin)
        ka = apply_rope(ka, cos, sin)
        out_a = diff_attention(qa, jnp.concatenate([ka, ka_c], axis=1),
                               jnp.concatenate([va, va_c], axis=1), lam, lam_init, subln_g[l])
        out_b = neighbourhood_attention(qb, kb, vb, kb_c, vb_c, na_rpb[l])
        x_new = x + gt1 * merge(out_a, out_b, ga, gb, w_branch_a[l], w_branch_b[l], w_out[l])

        x_new = x_new + gt2 * peer(modulate(x_new, norm2_g[l], sh2, sc2),
                                   peer_w_q[l], peer_subkeys[l], peer_u[l], peer_v[l])

        if l < DEPTH - 1:
            ca = diff_attention(qa_c, ka_c, va_c, lam, lam_init, subln_g[l])
            cb = full_attention(qb_c, kb_c, vb_c)
            ctx = ctx + gt1c * merge(ca, cb, ga_c, gb_c, w_branch_a[l], w_branch_b[l], w_out[l])
            ctx = ctx + gt2c * peer(modulate(ctx, norm2_g[l], sh2c, sc2c),
                                    peer_w_q[l], peer_subkeys[l], peer_u[l], peer_v[l])
        x = x_new
    return rmsnorm(x, final_g)
```

```python
import functools
import math

import numpy as np
import jax
import jax.numpy as jnp
from jax import lax
from jax.experimental import pallas as pl
from jax.experimental.pallas import tpu as pltpu

F32 = jnp.float32
BF16 = jnp.bfloat16

GRID_W = 64
A_HEADS = 8
A_QK_DIM = 64
A_V_DIM = 2 * A_QK_DIM
ROPE_THETA = 10000.0
B_HEADS = 8
B_HEAD_DIM = 128
NA_KH = 8
NA_KW = 16
PEER_HEADS = 8
PEER_NKEYS = 128
PEER_TOPK = 16
EPS = 1e-6
LAM_INIT = 0.8 - 0.6 * math.exp(-0.3 * 0)

LANES = 128
NA_ROWS_PER_BLOCK = 4
NA_WINDOW_BLOCKS = 3
MASK_VALUE = -1e30
V7X_VMEM_LIMIT = 56 * 1024 * 1024


def _cparams(sem):
    return pltpu.CompilerParams(dimension_semantics=sem, vmem_limit_bytes=V7X_VMEM_LIMIT)


def _ada_kernel(c_ref, w_ref, b_ref, o_ref):
    a = c_ref[...]
    a = a * jax.nn.sigmoid(a)
    o_ref[...] = jnp.dot(a, w_ref[...], preferred_element_type=F32,
                         precision=lax.Precision.HIGHEST) + b_ref[...]


def _ada(cond8, w, b):
    d, n = w.shape
    tn = 1024
    return pl.pallas_call(
        _ada_kernel,
        out_shape=jax.ShapeDtypeStruct((8, n), F32),
        grid=(n // tn,),
        in_specs=[pl.BlockSpec((8, d), lambda j: (0, 0)),
                  pl.BlockSpec((d, tn), lambda j: (0, j)),
                  pl.BlockSpec((1, tn), lambda j: (0, j))],
        out_specs=pl.BlockSpec((8, tn), lambda j: (0, j)),
        compiler_params=_cparams(("arbitrary",)),
        name="ada",
    )(cond8, w, b.reshape(1, n))


def _rope(x, cos, sin_a, sin_b):
    return (x * cos + pltpu.roll(x, LANES - 16, 1) * sin_a + pltpu.roll(x, 16, 1) * sin_b)


def _inproj_kernel(x_ref, g_ref, sc_ref, sh_ref, w_ref, cos_ref, sa_ref, sb_ref, *rest,
                   j0, with_q, tn):
    if with_q:
        qz_ref, kt_ref, p_ref, hx_scr = rest
    else:
        kt_ref, p_ref, hx_scr = rest
    jp = pl.program_id(1)
    j = jp + j0

    @pl.when(jp == 0)
    def _():
        x = x_ref[...]
        ms = jnp.mean(x * x, axis=-1, keepdims=True)
        y = x * lax.rsqrt(ms + EPS) * g_ref[...]
        hx_scr[...] = (y * (1.0 + sc_ref[...]) + sh_ref[...]).astype(BF16)

    acc = jnp.dot(hx_scr[...], w_ref[...], preferred_element_type=F32)
    ngrp = tn // LANES

    if with_q:
        @pl.when(j == 0)
        def _():
            lane = lax.broadcasted_iota(jnp.int32, (acc.shape[0], LANES), 1)
            first = lane < A_QK_DIM
            for g in range(ngrp):
                sl = slice(g * LANES, (g + 1) * LANES)
                r = _rope(acc[:, sl], cos_ref[...], sa_ref[...], sb_ref[...]) * (A_QK_DIM ** -0.5)
                qz_ref[0, :, sl] = jnp.where(first, r, 0.0).astype(BF16)
                qz_ref[1, :, sl] = jnp.where(first, 0.0, r).astype(BF16)

    @pl.when(j == 1)
    def _():
        for g in range(ngrp):
            sl = slice(g * LANES, (g + 1) * LANES)
            r = _rope(acc[:, sl], cos_ref[...], sa_ref[...], sb_ref[...])
            kt_ref[sl, :] = r.T.astype(BF16)

    @pl.when(jnp.logical_and(j >= 2, j < 6))
    def _():
        p_ref[...] = acc.astype(BF16)

    @pl.when(j >= 6)
    def _():
        p_ref[...] = jax.nn.sigmoid(acc).astype(BF16)


def _inproj(x2d, g, sc, sh, w_bf, cos, sin_a, sin_b, *, with_q, bm):
    t, d = x2d.shape
    n = w_bf.shape[1]
    tn = 1024
    nj = n // tn
    j0 = 0 if with_q else 1
    n_rest = n - 2 * tn
    out_shape = [jax.ShapeDtypeStruct((tn, t), BF16),
                 jax.ShapeDtypeStruct((t, n_rest), BF16)]
    out_specs = [pl.BlockSpec((tn, bm), lambda i, j: (0, i)),
                 pl.BlockSpec((bm, tn), lambda i, j: (i, jnp.maximum(j + j0 - 2, 0)))]
    if with_q:
        out_shape = [jax.ShapeDtypeStruct((2, t, tn), BF16)] + out_shape
        out_specs = [pl.BlockSpec((2, bm, tn), lambda i, j: (0, i, 0))] + out_specs
    row = lambda i, j: (0, 0)
    return pl.pallas_call(
        functools.partial(_inproj_kernel, j0=j0, with_q=with_q, tn=tn),
        out_shape=out_shape,
        grid=(t // bm, nj - j0),
        in_specs=[pl.BlockSpec((bm, d), lambda i, j: (i, 0)),
                  pl.BlockSpec((1, d), row), pl.BlockSpec((1, d), row), pl.BlockSpec((1, d), row),
                  pl.BlockSpec((d, tn), lambda i, j: (0, j + j0)),
                  pl.BlockSpec((bm, LANES), lambda i, j: (i, 0)),
                  pl.BlockSpec((bm, LANES), lambda i, j: (i, 0)),
                  pl.BlockSpec((bm, LANES), lambda i, j: (i, 0))],
        out_specs=out_specs,
        scratch_shapes=[pltpu.VMEM((bm, d), BF16)],
        compiler_params=_cparams(("arbitrary", "arbitrary")),
        name="inproj_q" if with_q else "inproj_ctx",
    )(x2d, g, sc, sh, w_bf, cos, sin_a, sin_b)


def _dattn_kernel(q_ref, kt_ref, v_ref, ktc_ref, vc_ref, lq1_ref, lk1_ref, lq2_ref, lk2_ref,
                  sg_ref, o_ref, m_scr, acc_scr, vext_scr, *, bq, kc, nchunk):
    ki = pl.program_id(2)
    q = q_ref[...].reshape(2 * bq, LANES)

    def update(kt, vext):
        s = jnp.dot(q, kt, preferred_element_type=F32)
        m_prev = m_scr[...]
        m_next = jnp.maximum(m_prev, jnp.max(s, axis=1, keepdims=True))
        p = jnp.exp(s - pltpu.repeat(m_next, s.shape[1] // LANES, 1))
        alpha = jnp.exp(m_prev - m_next)
        pv = jnp.dot(p.astype(BF16), vext, preferred_element_type=F32)
        acc_scr[...] = acc_scr[...] * pltpu.repeat(alpha, 2, 1) + pv
        m_scr[...] = m_next

    @pl.when(ki == 0)
    def _():
        m_scr[...] = jnp.full(m_scr.shape, -jnp.inf, F32)
        acc_scr[...] = jnp.zeros(acc_scr.shape, F32)
        vext_scr[:, LANES:] = jnp.ones((vext_scr.shape[0], LANES), BF16)
        nc = vc_ref.shape[0]
        vext_scr[0:nc, 0:LANES] = vc_ref[...]
        update(ktc_ref[...], vext_scr[0:nc, :])

    vext_scr[:, 0:LANES] = v_ref[...]
    for c in range(nchunk):
        update(kt_ref[:, c * kc:(c + 1) * kc], vext_scr[c * kc:(c + 1) * kc, :])

    @pl.when(ki == pl.num_programs(2) - 1)
    def _():
        lam = (jnp.exp(jnp.sum(lq1_ref[...] * lk1_ref[...], axis=1, keepdims=True))
               - jnp.exp(jnp.sum(lq2_ref[...] * lk2_ref[...], axis=1, keepdims=True)) + LAM_INIT)
        acc = acc_scr[...]
        o = acc[:, 0:LANES] / acc[:, LANES:]
        a = o[0:bq] - lam * o[bq:]
        ms = jnp.mean(a * a, axis=-1, keepdims=True)
        y = a * lax.rsqrt(ms + EPS) * sg_ref[...] * (1.0 - LAM_INIT)
        o_ref[...] = y.astype(BF16)


def _dattn(qz, kt, p, ktc, pc, lq1, lk1, lq2, lk2, sg, *, bq, bk, kc):
    t = qz.shape[1]
    c = ktc.shape[1]
    grid = (A_HEADS, t // bq, t // bk)
    vec = lambda h, i, k: (0, 0)
    return pl.pallas_call(
        functools.partial(_dattn_kernel, bq=bq, kc=kc, nchunk=bk // kc),
        out_shape=jax.ShapeDtypeStruct((t, A_HEADS * A_V_DIM), BF16),
        grid=grid,
        in_specs=[pl.BlockSpec((2, bq, LANES), lambda h, i, k: (0, i, h)),
                  pl.BlockSpec((LANES, bk), lambda h, i, k: (h, k)),
                  pl.BlockSpec((bk, LANES), lambda h, i, k: (k, h)),
                  pl.BlockSpec((LANES, c), lambda h, i, k: (h, 0)),
                  pl.BlockSpec((c, LANES), lambda h, i, k: (0, h)),
                  pl.BlockSpec((1, A_QK_DIM), vec), pl.BlockSpec((1, A_QK_DIM), vec),
                  pl.BlockSpec((1, A_QK_DIM), vec), pl.BlockSpec((1, A_QK_DIM), vec),
                  pl.BlockSpec((1, A_V_DIM), vec)],
        out_specs=pl.BlockSpec((bq, LANES), lambda h, i, k: (i, h)),
        scratch_shapes=[pltpu.VMEM((2 * bq, LANES), F32),
                        pltpu.VMEM((2 * bq, 2 * LANES), F32),
                        pltpu.VMEM((bk, 2 * LANES), BF16)],
        compiler_params=_cparams(("arbitrary", "arbitrary", "arbitrary")),
        name="dattn",
    )(qz, kt, p, ktc, pc, lq1, lk1, lq2, lk2, sg)


def _na_bias_table(rpb):
    r, w, nw = NA_ROWS_PER_BLOCK, GRID_W, NA_WINDOW_BLOCKS
    qi = np.arange(r * w)
    ki = np.arange(nw * r * w)
    qr, qc = qi // w, qi % w
    kr, kc = ki // w, ki % w
    cs = np.clip(qc - NA_KW // 2, 0, w - NA_KW)
    col_ok = (kc[None, :] >= cs[:, None]) & (kc[None, :] < cs[:, None] + NA_KW)
    dc = np.clip(kc[None, :] - qc[:, None] + (NA_KW - 1), 0, 2 * NA_KW - 2)
    tabs = []
    for q_off, rs_of in ((0, lambda rr: 0 * rr), (r, lambda rr: rr - NA_KH // 2),
                         (2 * r, lambda rr: 0 * rr + r)):
        q_abs = qr + q_off
        rs = rs_of(q_abs)
        row_ok = (kr[None, :] >= rs[:, None]) & (kr[None, :] < rs[:, None] + NA_KH)
        dr = np.clip(kr[None, :] - q_abs[:, None] + (NA_KH - 1), 0, 2 * NA_KH - 2)
        ok = row_ok & col_ok
        tabs.append(jnp.where(ok[None], rpb[:, dr, dc], MASK_VALUE))
    return jnp.stack(tabs, axis=1).astype(F32)


def _nattn_kernel(q_ref, k0_ref, k1_ref, k2_ref, v0_ref, v1_ref, v2_ref, kc_ref, vc_ref,
                  b_ref, o_ref):
    scale = B_HEAD_DIM ** -0.5
    nt = (((1,), (1,)), ((), ()))
    q = q_ref[...]
    ks = (k0_ref, k1_ref, k2_ref)
    vs = (v0_ref, v1_ref, v2_ref)
    bq = q.shape[0]
    s_parts = []
    for d in range(NA_WINDOW_BLOCKS):
        s = lax.dot_general(q, ks[d][...], nt, preferred_element_type=F32) * scale
        s_parts.append(s + b_ref[0, 0, :, d * bq:(d + 1) * bq])
    s_parts.append(lax.dot_general(q, kc_ref[...], nt, preferred_element_type=F32) * scale)
    m = s_parts[0].max(axis=1, keepdims=True)
    for s in s_parts[1:]:
        m = jnp.maximum(m, s.max(axis=1, keepdims=True))
    ps = [jnp.exp(s - m) for s in s_parts]
    l = ps[0].sum(axis=1, keepdims=True)
    for pp in ps[1:]:
        l = l + pp.sum(axis=1, keepdims=True)
    inv = 1.0 / l
    o = jnp.dot((ps[-1] * inv).astype(BF16), vc_ref[...], preferred_element_type=F32)
    for d in range(NA_WINDOW_BLOCKS):
        o = o + jnp.dot((ps[d] * inv).astype(BF16), vs[d][...], preferred_element_type=F32)
    o_ref[...] = o.astype(BF16)


def _nattn(p, pc, bias):
    t = p.shape[0]
    c = pc.shape[0]
    bq = NA_ROWS_PER_BLOCK * GRID_W
    nb = t // bq
    qcol = A_HEADS * A_V_DIM // LANES
    kcol = qcol + B_HEADS * B_HEAD_DIM // LANES
    vcol = kcol + B_HEADS * B_HEAD_DIM // LANES

    def wb(b):
        return jnp.clip(b - 1, 0, nb - NA_WINDOW_BLOCKS)

    def kspec(d, col):
        return pl.BlockSpec((bq, LANES), lambda h, b: (wb(b) + d, col + h))

    def btype(b):
        return jnp.where(b == 0, 0, jnp.where(b == nb - 1, 2, 1))

    return pl.pallas_call(
        _nattn_kernel,
        out_shape=jax.ShapeDtypeStruct((t, B_HEADS * B_HEAD_DIM), BF16),
        grid=(B_HEADS, nb),
        in_specs=[pl.BlockSpec((bq, LANES), lambda h, b: (b, qcol + h)),
                  kspec(0, kcol), kspec(1, kcol), kspec(2, kcol),
                  kspec(0, vcol), kspec(1, vcol), kspec(2, vcol),
                  pl.BlockSpec((c, LANES), lambda h, b: (0, kcol + h)),
                  pl.BlockSpec((c, LANES), lambda h, b: (0, vcol + h)),
                  pl.BlockSpec((1, 1, bq, NA_WINDOW_BLOCKS * bq), lambda h, b: (h, btype(b), 0, 0))],
        out_specs=pl.BlockSpec((bq, LANES), lambda h, b: (b, h)),
        compiler_params=_cparams(("arbitrary", "arbitrary")),
        name="nattn",
    )(p, p, p, p, p, p, p, pc, pc, bias)


def _merge_kernel(oa_ref, ob_ref, ga_ref, gb_ref, x_ref, wa_ref, wb_ref, wo_ref,
                  gt_ref, g2_ref, sc_ref, sh_ref, xn_ref, ht_ref):
    ta = jnp.dot(oa_ref[...], wa_ref[...], preferred_element_type=F32)
    tb = jnp.dot(ob_ref[...], wb_ref[...], preferred_element_type=F32)
    mrg = ga_ref[...].astype(F32) * ta + gb_ref[...].astype(F32) * tb
    y = jnp.dot(mrg.astype(BF16), wo_ref[...], preferred_element_type=F32)
    xn = x_ref[...] + gt_ref[...] * y
    xn_ref[...] = xn
    ms = jnp.mean(xn * xn, axis=-1, keepdims=True)
    h2 = xn * lax.rsqrt(ms + EPS) * g2_ref[...]
    h2 = h2 * (1.0 + sc_ref[...]) + sh_ref[...]
    ht_ref[...] = h2.T.astype(BF16)


def _merge(oa, ob, p, x2d, wa, wb, wo, gt1, g2, sc2, sh2, *, bm):
    t, d = x2d.shape
    wa_n = oa.shape[1]
    nga = (A_HEADS * A_V_DIM + 3 * B_HEADS * B_HEAD_DIM) // d
    row = lambda i: (0, 0)
    const = dict(pipeline_mode=pl.Buffered(1))
    return pl.pallas_call(
        _merge_kernel,
        out_shape=[jax.ShapeDtypeStruct((t, d), F32), jax.ShapeDtypeStruct((d, t), BF16)],
        grid=(t // bm,),
        in_specs=[pl.BlockSpec((bm, wa_n), lambda i: (i, 0)),
                  pl.BlockSpec((bm, wa_n), lambda i: (i, 0)),
                  pl.BlockSpec((bm, d), lambda i: (i, nga)),
                  pl.BlockSpec((bm, d), lambda i: (i, nga + 1)),
                  pl.BlockSpec((bm, d), lambda i: (i, 0)),
                  pl.BlockSpec((wa_n, d), row, **const),
                  pl.BlockSpec((wa_n, d), row, **const),
                  pl.BlockSpec((d, d), row, **const),
                  pl.BlockSpec((1, d), row), pl.BlockSpec((1, d), row),
                  pl.BlockSpec((1, d), row), pl.BlockSpec((1, d), row)],
        out_specs=[pl.BlockSpec((bm, d), lambda i: (i, 0)),
                   pl.BlockSpec((d, bm), lambda i: (0, i))],
        compiler_params=_cparams(("arbitrary",)),
        name="merge",
    )(oa, ob, p, p, x2d, wa, wb, wo, gt1, g2, sc2, sh2)


def _top16(x):
    bn = x.shape[1]
    slot = lax.broadcasted_iota(jnp.int32, (PEER_TOPK, bn), 0).astype(F32)

    def body(_, carry):
        x, top, filled = carry
        mx = jnp.max(x, axis=0, keepdims=True)
        eq = x == mx
        cnt = jnp.sum(jnp.where(eq, 1.0, 0.0), axis=0, keepdims=True)
        take = jnp.logical_and(slot >= filled, slot < filled + cnt)
        top = jnp.where(take, mx, top)
        x = jnp.where(eq, -jnp.inf, x)
        return x, top, filled + cnt

    init = (x, jnp.full((PEER_TOPK, bn), -jnp.inf, F32), jnp.zeros((1, bn), F32))
    return lax.fori_loop(0, PEER_TOPK, body, init)[1]


def _pquery_kernel(ht_ref, wqt_ref, sk_ref, s1_ref, s2_ref, st_ref):
    qt = jnp.dot(wqt_ref[...], ht_ref[...], preferred_element_type=F32).astype(BF16)
    nk = PEER_NKEYS
    for h in range(PEER_HEADS):
        tops = []
        for half, s_ref in ((0, s1_ref), (1, s2_ref)):
            r0 = (2 * h + half) * nk
            s = jnp.dot(sk_ref[2 * h + half], qt[r0:r0 + nk, :], preferred_element_type=F32)
            s_ref[h] = s
            tops.append(_top16(s))
        t1, t2 = tops
        cand = jnp.concatenate([t1[i:i + 1, :] + t2 for i in range(PEER_TOPK)], axis=0)
        tc = _top16(cand)
        z = jnp.sum(jnp.exp(tc - tc[0:1, :]), axis=0, keepdims=True)
        st_ref[0, h:h + 1, :] = t1[0:1, :]
        st_ref[1, h:h + 1, :] = t2[0:1, :]
        st_ref[2, h:h + 1, :] = tc[PEER_TOPK - 1:PEER_TOPK, :]
        st_ref[3, h:h + 1, :] = 1.0 / z


def _pquery(ht, wqt, sk, *, bn):
    d, t = ht.shape
    nq = wqt.shape[0]
    hh, nk = PEER_HEADS, PEER_NKEYS
    const = dict(pipeline_mode=pl.Buffered(1))
    return pl.pallas_call(
        _pquery_kernel,
        out_shape=[jax.ShapeDtypeStruct((hh, nk, t), F32),
                   jax.ShapeDtypeStruct((hh, nk, t), F32),
                   jax.ShapeDtypeStruct((4, hh, t), F32)],
        grid=(t // bn,),
        in_specs=[pl.BlockSpec((d, bn), lambda i: (0, i)),
                  pl.BlockSpec((nq, d), lambda i: (0, 0), **const),
                  pl.BlockSpec((2 * hh, nk, sk.shape[2]), lambda i: (0, 0, 0), **const)],
        out_specs=[pl.BlockSpec((hh, nk, bn), lambda i: (0, 0, i)),
                   pl.BlockSpec((hh, nk, bn), lambda i: (0, 0, i)),
                   pl.BlockSpec((4, hh, bn), lambda i: (0, 0, i))],
        compiler_params=_cparams(("arbitrary",)),
        name="pquery",
    )(ht, wqt, sk)


def _gelu_tanh(x):
    c = math.sqrt(2.0 / math.pi)
    return 0.5 * x * (1.0 + jnp.tanh(c * (x + 0.044715 * (x * x * x))))


def _pexpert_kernel(ht_ref, u_ref, vt_ref, s1_ref, s2_ref, st_ref, o_ref,
                    act_scr, g_scr, e2_scr, *, na):
    j = pl.program_id(1)
    nk = PEER_NKEYS

    @pl.when(j == 0)
    def _():
        o_ref[...] = jnp.zeros(o_ref.shape, F32)
        for h in range(PEER_HEADS):
            e2_scr[h] = jnp.exp(s2_ref[h] - st_ref[1, h:h + 1, :]) * st_ref[3, h:h + 1, :]

    act_scr[...] = jnp.dot(u_ref[...], ht_ref[...], preferred_element_type=F32)

    def body(a, carry):
        w = jnp.zeros((nk, act_scr.shape[1]), F32)
        for h in range(PEER_HEADS):
            s1 = s1_ref[h, pl.ds(a, 1), :]
            e1 = jnp.exp(s1 - st_ref[0, h:h + 1, :])
            keep = (s1 + s2_ref[h]) >= st_ref[2, h:h + 1, :]
            w = w + jnp.where(keep, e2_scr[h] * e1, 0.0)
        rows = pl.ds(pl.multiple_of(a * nk, nk), nk)
        g_scr[rows, :] = (w * _gelu_tanh(act_scr[rows, :])).astype(BF16)
        return carry

    lax.fori_loop(0, na, body, 0)
    o_ref[...] += jnp.dot(vt_ref[...], g_scr[...], preferred_element_type=F32)


def _pexpert(ht, u_bf, vt_bf, s1t, s2t, st, *, bn, na):
    d, t = ht.shape
    e = u_bf.shape[0]
    hh, nk = PEER_HEADS, PEER_NKEYS
    be = na * nk
    return pl.pallas_call(
        functools.partial(_pexpert_kernel, na=na),
        out_shape=jax.ShapeDtypeStruct((d, t), F32),
        grid=(t // bn, e // be),
        in_specs=[pl.BlockSpec((d, bn), lambda i, j: (0, i)),
                  pl.BlockSpec((be, d), lambda i, j: (j, 0)),
                  pl.BlockSpec((d, be), lambda i, j: (0, j)),
                  pl.BlockSpec((hh, na, bn), lambda i, j: (0, j, i)),
                  pl.BlockSpec((hh, nk, bn), lambda i, j: (0, 0, i)),
                  pl.BlockSpec((4, hh, bn), lambda i, j: (0, 0, i))],
        out_specs=pl.BlockSpec((d, bn), lambda i, j: (0, i)),
        scratch_shapes=[pltpu.VMEM((be, bn), F32),
                        pltpu.VMEM((be, bn), BF16),
                        pltpu.VMEM((hh, nk, bn), F32)],
        compiler_params=_cparams(("arbitrary", "arbitrary")),
        name="pexpert",
    )(ht, u_bf, vt_bf, s1t, s2t, st)


def _final_kernel(xn_ref, pt_ref, gt_ref, g_ref, o_ref):
    x = xn_ref[...] + gt_ref[...] * pt_ref[...].T
    ms = jnp.mean(x * x, axis=-1, keepdims=True)
    o_ref[...] = x * lax.rsqrt(ms + EPS) * g_ref[...]


def _final(xn, pt, gt2, g, *, bm):
    t, d = xn.shape
    row = lambda i: (0, 0)
    return pl.pallas_call(
        _final_kernel,
        out_shape=jax.ShapeDtypeStruct((t, d), F32),
        grid=(t // bm,),
        in_specs=[pl.BlockSpec((bm, d), lambda i: (i, 0)),
                  pl.BlockSpec((d, bm), lambda i: (0, i)),
                  pl.BlockSpec((1, d), row), pl.BlockSpec((1, d), row)],
        out_specs=pl.BlockSpec((bm, d), lambda i: (i, 0)),
        compiler_params=_cparams(("arbitrary",)),
        name="final",
    )(xn, pt, gt2, g)


def _rope_tables(seq_len):
    t = jnp.arange(seq_len)
    row = (t // GRID_W).astype(F32)
    col = (t % GRID_W).astype(F32)
    n_freq = A_QK_DIM // 4
    freqs = ROPE_THETA ** (-jnp.arange(n_freq, dtype=F32) / n_freq)
    ar = row[:, None] * freqs[None, :]
    ac = col[:, None] * freqs[None, :]
    ang = jnp.concatenate([ar, ar, ac, ac] * 2, axis=-1)
    first = (np.arange(LANES) % (2 * n_freq)) < n_freq
    sin = jnp.sin(ang)
    return jnp.cos(ang), jnp.where(first, -sin, 0.0), jnp.where(first, 0.0, sin)


def kernel(x, c, ctx, c_ctx, w_mod, b_mod, norm1_g, norm2_g, w_in, lambda_q1, lambda_k1,
           lambda_q2, lambda_k2, subln_g, na_rpb, w_branch_a, w_branch_b, w_out, peer_w_q,
           peer_subkeys, peer_u, peer_v, final_g):
    bsz, t, d = x.shape
    n_ctx = ctx.shape[1]
    assert bsz == 1 and w_mod.shape[0] == 1, "single sample, single layer"
    rows = t // GRID_W
    assert t % (NA_ROWS_PER_BLOCK * GRID_W) == 0 and rows >= NA_ROWS_PER_BLOCK * NA_WINDOW_BLOCKS
    assert n_ctx % LANES == 0

    cond = jnp.concatenate([c, c_ctx[None, :], jnp.zeros((6, d), F32)], axis=0)
    mod = _ada(cond, w_mod[0], b_mod[0])
    sh1, sc1, gt1, sh2, sc2, gt2 = [mod[0:1, k * d:(k + 1) * d] for k in range(6)]
    sh1c, sc1c = mod[1:2, 0:d], mod[1:2, d:2 * d]

    w_in_bf = w_in[0].astype(BF16)
    g1 = norm1_g[0][None, :]
    cos, sin_a, sin_b = _rope_tables(t)
    bm_in = min(512, t)
    qz, kt, p = _inproj(x[0], g1, sc1, sh1, w_in_bf, cos, sin_a, sin_b, with_q=True, bm=bm_in)
    ones = jnp.ones((n_ctx, LANES), F32)
    zeros = jnp.zeros((n_ctx, LANES), F32)
    ktc, pc = _inproj(ctx[0], g1, sc1c, sh1c, w_in_bf, ones, zeros, zeros, with_q=False, bm=n_ctx)

    bq = min(512, t)
    bk = min(2048, t)
    out_a = _dattn(qz, kt, p, ktc, pc, lambda_q1, lambda_k1, lambda_q2, lambda_k2, subln_g,
                   bq=bq, bk=bk, kc=min(512, bk))
    out_b = _nattn(p, pc, _na_bias_table(na_rpb[0]))

    xn, ht = _merge(out_a, out_b, p, x[0], w_branch_a[0].astype(BF16), w_branch_b[0].astype(BF16),
                    w_out[0].astype(BF16), gt1, norm2_g[0][None, :], sc2, sh2, bm=min(256, t))

    nk = PEER_NKEYS
    wqt = peer_w_q[0].T.astype(BF16)
    sk = peer_subkeys[0].reshape(2 * PEER_HEADS, nk, -1).astype(BF16)
    s1t, s2t, st = _pquery(ht, wqt, sk, bn=min(512, t))
    pt = _pexpert(ht, peer_u[0].astype(BF16), peer_v[0].T.astype(BF16), s1t, s2t, st,
                  bn=min(512, t), na=8)
    out = _final(xn, pt, gt2, final_g[None, :], bm=min(256, t))
    return out[None]
```
